```python
import jax, jax.numpy as jnp
from jax import lax
import numpy as np

D_MODEL = 1024
BATCH = 4
SEQ = 4096
DEPTH = 2
DEC_BATCH = 8
DEC_SEQ = 32
PAST_LEN = 1024

CHUNK = 64
N_MIXERS = 2
N_POOL_LAYERS = (DEPTH + 1) // 2
N_ATTN_LAYERS = DEPTH // 2
POOL_WINDOWS = (2, 4, 8, 16)
N_POOL_GROUPS = 4
POOL_GROUP = D_MODEL // N_POOL_GROUPS
POOL_HIST = max(POOL_WINDOWS) - 1
N_HEADS = 16
HEAD_DIM = D_MODEL // N_HEADS
LEFT_CHUNKS = 8
BAND_CHUNKS = LEFT_CHUNKS + 1
REL_MAX = 256
N_REL = CHUNK + REL_MAX
N_GROUPS = 4
EXPERTS_PER_GROUP = 8
N_EXPERTS = N_GROUPS * EXPERTS_PER_GROUP
TOP_K = 2
D_EXPERT = D_MODEL // 2
MOE_BLOCK = 128
EPS = 1e-6

kernel_name = "streaming_pool_bandattn_hmoe"

f32 = jnp.float32


def rms_norm(x, g):
    xf = x.astype(f32)
    y = xf * lax.rsqrt(jnp.mean(xf * xf, axis=-1, keepdims=True) + EPS)
    return (y * g.astype(f32)).astype(x.dtype)


def pool_mix(h, hist, pos0, w_pool, pool_scale):
    B, L, D = h.shape
    ext = jnp.concatenate([hist.astype(h.dtype), h], axis=1)
    cs = jnp.cumsum(ext.astype(f32), axis=1)
    cs = jnp.concatenate([jnp.zeros((B, 1, D), f32), cs], axis=1)
    end = cs[:, POOL_HIST + 1:]
    pos = pos0 + jnp.arange(L, dtype=jnp.int32)
    outs = []
    for g, w in enumerate(POOL_WINDOWS):
        sl = slice(g * POOL_GROUP, (g + 1) * POOL_GROUP)
        start = cs[:, POOL_HIST + 1 - w: POOL_HIST + 1 - w + L, sl]
        cnt = jnp.minimum(pos + 1, w).astype(f32)[None, :, None]
        outs.append((end[..., sl] - start) / cnt)
    pooled = jnp.stack(outs, axis=2)
    delta = pooled - h.reshape(B, L, N_POOL_GROUPS, POOL_GROUP).astype(f32)
    y = jnp.einsum('blgc,gcd->blgd', delta, w_pool.astype(f32)).reshape(B, L, D)
    y = (y * pool_scale.astype(f32)).astype(h.dtype)
    return y, ext[:, -POOL_HIST:]


def attn_project(h, w_qkv, q_gain, k_gain):
    B, L, _ = h.shape
    qkv = jnp.einsum('bld,de->ble', h, w_qkv).reshape(B, L, 3, N_HEADS, HEAD_DIM)
    q = rms_norm(qkv[:, :, 0], q_gain)
    k = rms_norm(qkv[:, :, 1], k_gain)
    v = qkv[:, :, 2]
    return q, k, v


def band_attention(q, k, v, qpos, kpos, rel_bias):
    qc = qpos[:, :, None] // CHUNK
    kc = kpos[:, None, :] // CHUNK
    visible = (kpos[:, None, :] >= 0) & (kc <= qc) & (qc - kc <= LEFT_CHUNKS)
    rel = qpos[:, :, None] - kpos[:, None, :]
    idx = jnp.clip(rel, -(CHUNK - 1), REL_MAX) + (CHUNK - 1)
    bias = jnp.moveaxis(rel_bias[:, idx], 0, 1).astype(f32)
    s = jnp.einsum('bnqhd,bnkhd->bnhqk', q.astype(f32), k.astype(f32)) * (HEAD_DIM ** -0.5) + bias[None]
    s = jnp.where(visible[None, :, None], s, -1e30)
    p = jax.nn.softmax(s, axis=-1)
    return jnp.einsum('bnhqk,bnkhd->bnqhd', p, v.astype(f32)).astype(q.dtype)


def attn_prompt(h, w_qkv, w_o, q_gain, k_gain, rel_bias):
    B, S, D = h.shape
    q, k, v = attn_project(h, w_qkv, q_gain, k_gain)
    nc = S // CHUNK
    pad = LEFT_CHUNKS * CHUNK

    def band(t):
        tp = jnp.pad(t, ((0, 0), (pad, 0), (0, 0), (0, 0))).reshape(B, nc + LEFT_CHUNKS, CHUNK, N_HEADS, HEAD_DIM)
        return jnp.stack([tp[:, j:j + nc] for j in range(BAND_CHUNKS)], axis=2).reshape(
            B, nc, BAND_CHUNKS * CHUNK, N_HEADS, HEAD_DIM)

    qpos = jnp.arange(S, dtype=jnp.int32).reshape(nc, CHUNK)
    kpos = (jnp.arange(nc, dtype=jnp.int32)[:, None] * CHUNK - pad) + jnp.arange(BAND_CHUNKS * CHUNK, dtype=jnp.int32)[None, :]
    o = band_attention(q.reshape(B, nc, CHUNK, N_HEADS, HEAD_DIM), band(k), band(v), qpos, kpos, rel_bias)
    y = jnp.einsum('bse,ed->bsd', o.reshape(B, S, D), w_o)
    keep = min(LEFT_CHUNKS * CHUNK, S)
    return y, k[:, S - keep:], v[:, S - keep:]


def attn_sample(h, cache_k, cache_v, w_qkv, w_o, q_gain, k_gain, rel_bias):
    B, L, D = h.shape
    q, k, v = attn_project(h, w_qkv, q_gain, k_gain)
    keep = cache_k.shape[1]
    kk = jnp.concatenate([cache_k.astype(k.dtype), k], axis=1)[:, None]
    vv = jnp.concatenate([cache_v.astype(v.dtype), v], axis=1)[:, None]
    qpos = (PAST_LEN + jnp.arange(L, dtype=jnp.int32))[None]
    kpos = (PAST_LEN - keep + jnp.arange(keep + L, dtype=jnp.int32))[None]
    o = band_attention(q[:, None], kk, vv, qpos, kpos, rel_bias)
    y = jnp.einsum('bse,ed->bsd', o.reshape(B, L, D), w_o)
    return y, k, v


def hier_moe(x, w_r1, b_r1, w_r2, b_r2, w_gate, w_up, w_down):
    B, L, D = x.shape
    xt = x.reshape(-1, D)
    T = xt.shape[0]
    logit1 = (xt @ w_r1).astype(f32) + b_r1.astype(f32)
    p1 = jax.nn.softmax(logit1, axis=-1)
    grp = jnp.argmax(logit1, axis=-1)
    p_grp = jnp.max(p1, axis=-1, keepdims=True)
    logit2_all = jnp.einsum('td,gde->tge', xt, w_r2).astype(f32) + b_r2.astype(f32)
    logit2 = jnp.einsum('tg,tge->te', jax.nn.one_hot(grp, N_GROUPS, dtype=f32), logit2_all)
    p2 = jax.nn.softmax(logit2, axis=-1)
    top_p, top_e = lax.top_k(p2, TOP_K)
    gates = p_grp * top_p / jnp.sum(top_p, axis=-1, keepdims=True)
    expert = grp[:, None].astype(jnp.int32) * EXPERTS_PER_GROUP + top_e.astype(jnp.int32)

    A = T * TOP_K
    n_blocks = -(-A // MOE_BLOCK) + N_EXPERTS
    P = n_blocks * MOE_BLOCK
    flat_e = expert.reshape(-1)
    flat_tok = jnp.repeat(jnp.arange(T, dtype=jnp.int32), TOP_K)
    flat_g = gates.reshape(-1)
    order = jnp.argsort(flat_e)
    se, stok, sg = flat_e[order], flat_tok[order], flat_g[order]
    counts = jnp.bincount(flat_e, length=N_EXPERTS).astype(jnp.int32)
    start = jnp.cumsum(counts) - counts
    pcounts = (counts + MOE_BLOCK - 1) // MOE_BLOCK * MOE_BLOCK
    pend = jnp.cumsum(pcounts)
    pstart = pend - pcounts
    dest = pstart[se] + (jnp.arange(A, dtype=jnp.int32) - start[se])
    row_tok = jnp.full((P,), T, jnp.int32).at[dest].set(stok)
    row_gate = jnp.zeros((P,), f32).at[dest].set(sg)
    blk_expert = jnp.minimum(jnp.searchsorted(pend, jnp.arange(n_blocks, dtype=jnp.int32) * MOE_BLOCK, side='right'),
                             N_EXPERTS - 1).astype(jnp.int32)
    xpad = jnp.concatenate([xt, jnp.zeros((1, D), xt.dtype)], axis=0)

    def run_block(args):
        tok, e = args
        xb = xpad[tok]
        hidden = jax.nn.silu(xb @ w_gate[e]) * (xb @ w_up[e])
        return hidden @ w_down[e]

    yb = lax.map(run_block, (row_tok.reshape(n_blocks, MOE_BLOCK), blk_expert))
    y = jax.ops.segment_sum(yb.reshape(P, D).astype(f32) * row_gate[:, None], row_tok, num_segments=T + 1)[:T]
    return y.reshape(B, L, D).astype(x.dtype)


def setup_inputs(seed: int = 0) -> dict:
    key = jax.random.key(seed)
    ks = jax.random.split(key, 24)

    def nrm(k, shape, scale):
        return scale * jax.random.normal(k, shape, f32)

    kv_keep = min(LEFT_CHUNKS * CHUNK, PAST_LEN)
    return {
        "x_prompt": nrm(ks[0], (BATCH, SEQ, D_MODEL), 1.0),
        "x_sample": nrm(ks[1], (DEC_BATCH, DEC_SEQ, D_MODEL), 1.0),
        "cache_pool": nrm(ks[2], (N_POOL_LAYERS, DEC_BATCH, POOL_HIST, D_MODEL), 1.0),
        "cache_k": nrm(ks[3], (N_ATTN_LAYERS, DEC_BATCH, kv_keep, N_HEADS, HEAD_DIM), 1.0),
        "cache_v": nrm(ks[4], (N_ATTN_LAYERS, DEC_BATCH, kv_keep, N_HEADS, HEAD_DIM), 1.0),
        "norm_mix": 1.0 + nrm(ks[5], (DEPTH, D_MODEL), 0.02),
        "norm_ffn": 1.0 + nrm(ks[6], (DEPTH, D_MODEL), 0.02),
        "pool_w": nrm(ks[7], (N_POOL_LAYERS, N_POOL_GROUPS, POOL_GROUP, POOL_GROUP), POOL_GROUP ** -0.5),
        "pool_scale": 0.3 + nrm(ks[8], (N_POOL_LAYERS, D_MODEL), 0.05),
        "attn_w_qkv": nrm(ks[9], (N_ATTN_LAYERS, D_MODEL, 3 * D_MODEL), D_MODEL ** -0.5),
        "attn_w_o": nrm(ks[10], (N_ATTN_LAYERS, D_MODEL, D_MODEL), D_MODEL ** -0.5),
        "attn_q_gain": 1.0 + nrm(ks[11], (N_ATTN_LAYERS, HEAD_DIM), 0.02),
        "attn_k_gain": 1.0 + nrm(ks[12], (N_ATTN_LAYERS, HEAD_DIM), 0.02),
        "attn_rel_bias": nrm(ks[13], (N_ATTN_LAYERS, N_HEADS, N_REL), 0.1),
        "moe_w_r1": nrm(ks[14], (DEPTH, D_MODEL, N_GROUPS), D_MODEL ** -0.5),
        "moe_b_r1": nrm(ks[15], (DEPTH, N_GROUPS), 0.01),
        "moe_w_r2": nrm(ks[16], (DEPTH, N_GROUPS, D_MODEL, EXPERTS_PER_GROUP), D_MODEL ** -0.5),
        "moe_b_r2": nrm(ks[17], (DEPTH, N_GROUPS, EXPERTS_PER_GROUP), 0.01),
        "moe_w_gate": nrm(ks[18], (DEPTH, N_EXPERTS, D_MODEL, D_EXPERT), D_MODEL ** -0.5),
        "moe_w_up": nrm(ks[19], (DEPTH, N_EXPERTS, D_MODEL, D_EXPERT), D_MODEL ** -0.5),
        "moe_w_down": nrm(ks[20], (DEPTH, N_EXPERTS, D_EXPERT, D_MODEL), D_EXPERT ** -0.5),
    }


def reference(x_prompt, x_sample, cache_pool, cache_k, cache_v, norm_mix, norm_ffn, pool_w, pool_scale,
              attn_w_qkv, attn_w_o, attn_q_gain, attn_k_gain, attn_rel_bias,
              moe_w_r1, moe_b_r1, moe_w_r2, moe_b_r2, moe_w_gate, moe_w_up, moe_w_down):
    xp, xs = x_prompt, x_sample
    pool_new_p, pool_new_s, k_new_p, v_new_p, k_new_s, v_new_s = [], [], [], [], [], []
    for i in range(DEPTH):
        hp = rms_norm(xp, norm_mix[i])
        hs = rms_norm(xs, norm_mix[i])
        if i % N_MIXERS == 0:
            a = i // N_MIXERS
            zero_hist = jnp.zeros((hp.shape[0], POOL_HIST, D_MODEL), hp.dtype)
            yp, stp = pool_mix(hp, zero_hist, 0, pool_w[a], pool_scale[a])
            ys, sts = pool_mix(hs, cache_pool[a], PAST_LEN, pool_w[a], pool_scale[a])
            pool_new_p.append(stp)
            pool_new_s.append(sts)
        else:
            a = i // N_MIXERS
            yp, kp, vp = attn_prompt(hp, attn_w_qkv[a], attn_w_o[a], attn_q_gain[a], attn_k_gain[a], attn_rel_bias[a])
            ys, ks_, vs_ = attn_sample(hs, cache_k[a], cache_v[a], attn_w_qkv[a], attn_w_o[a],
                                       attn_q_gain[a], attn_k_gain[a], attn_rel_bias[a])
            k_new_p.append(kp)
            v_new_p.append(vp)
            k_new_s.append(ks_)
            v_new_s.append(vs_)
        xp = xp + yp
        xs = xs + ys
        moe_args = (moe_w_r1[i], moe_b_r1[i], moe_w_r2[i], moe_b_r2[i], moe_w_gate[i], moe_w_up[i], moe_w_down[i])
        xp = xp + hier_moe(rms_norm(xp, norm_ffn[i]), *moe_args)
        xs = xs + hier_moe(rms_norm(xs, norm_ffn[i]), *moe_args)
    state_pool_prompt = jnp.stack(pool_new_p, axis=0)
    state_pool_sample = jnp.stack(pool_new_s, axis=0)
    k_prompt = jnp.stack(k_new_p, axis=0)
    v_prompt = jnp.stack(v_new_p, axis=0)
    k_sample = jnp.stack(k_new_s, axis=0)
    v_sample = jnp.stack(v_new_s, axis=0)
    return (xp, xs, state_pool_prompt, state_pool_sample, k_prompt, v_prompt, k_sample, v_sample)
```

```python
import functools

import numpy as np
import jax
import jax.numpy as jnp
from jax import lax
from jax.experimental import pallas as pl
from jax.experimental.pallas import tpu as pltpu

f32 = jnp.float32
bf16 = jnp.bfloat16
i32 = jnp.int32

EPS = 1e-6
CHUNK = 64
LEFT_CHUNKS = 8
BAND = (LEFT_CHUNKS + 1) * CHUNK
REL_MAX = 256
POOL_WINDOWS = (2, 4, 8, 16)
POOL_HIST = 15
HIST_ROWS = 16
N_HEADS = 16
N_GROUPS = 4
EXPERTS_PER_GROUP = 8
N_EXPERTS = N_GROUPS * EXPERTS_PER_GROUP
TOP_K = 2
LANES = 128
ROUTE_COL0 = N_GROUPS

TOKEN_TILE = 256
MOE_ROWS = 256
VMEM_LIMIT = 56 * 1024 * 1024


def _params(sem, vmem=VMEM_LIMIT, **kw):
    return pltpu.CompilerParams(dimension_semantics=sem, vmem_limit_bytes=vmem, **kw)


def _rms(x, g):
    return x * lax.rsqrt(jnp.mean(x * x, axis=-1, keepdims=True) + EPS) * g


def _pool_kernel(x_ref, hist_ref, g_ref, a_ref, ah_ref, w_ref, sc_ref, o_ref, hl_ref, hist_sc, *, ts, pos0):
    s = pl.program_id(1)

    @pl.when(s == 0)
    def _():
        hist_sc[...] = hist_ref[0]

    x = x_ref[0]
    h = _rms(x, g_ref[...])
    hb = h.astype(bf16)
    hist_b = hist_sc[...].astype(bf16)
    pos = pos0 + s * ts + lax.broadcasted_iota(i32, (ts, 1), 0)
    gw = h.shape[1] // len(POOL_WINDOWS)
    ys = []
    for g, w in enumerate(POOL_WINDOWS):
        sl = slice(g * gw, (g + 1) * gw)
        tot = jnp.dot(a_ref[g], hb[:, sl], preferred_element_type=f32)
        tot = tot + jnp.dot(ah_ref[g], hist_b[:, sl], preferred_element_type=f32)
        cnt = jnp.minimum(pos + 1, w).astype(f32)
        delta = tot / cnt - h[:, sl]
        ys.append(jnp.dot(delta.astype(bf16), w_ref[g], preferred_element_type=f32))
    y = jnp.concatenate(ys, axis=1) * sc_ref[...]
    o_ref[0] = x + y
    tail = h[ts - HIST_ROWS:, :]
    hist_sc[...] = tail
    hl_ref[0] = tail


def _pool_layer(x, hist, g, w_pool_b, scale, pos0):
    B, S, D = x.shape
    ts = min(S, TOKEN_TILE)
    r = np.arange(ts)[:, None]
    c = np.arange(ts)[None, :]
    q = np.arange(HIST_ROWS)[None, :]
    a = np.stack([((r - c >= 0) & (r - c < w)) for w in POOL_WINDOWS]).astype(np.float32)
    ah = np.stack([(r + HIST_ROWS - q < w) for w in POOL_WINDOWS]).astype(np.float32)
    ng = len(POOL_WINDOWS)
    gw = D // ng
    return pl.pallas_call(
        functools.partial(_pool_kernel, ts=ts, pos0=pos0),
        grid=(B, S // ts),
        in_specs=[
            pl.BlockSpec((1, ts, D), lambda b, s: (b, s, 0)),
            pl.BlockSpec((1, HIST_ROWS, D), lambda b, s: (b, 0, 0)),
            pl.BlockSpec((1, D), lambda b, s: (0, 0)),
            pl.BlockSpec((ng, ts, ts), lambda b, s: (0, 0, 0)),
            pl.BlockSpec((ng, ts, HIST_ROWS), lambda b, s: (0, 0, 0)),
            pl.BlockSpec((ng, gw, gw), lambda b, s: (0, 0, 0)),
            pl.BlockSpec((1, D), lambda b, s: (0, 0)),
        ],
        out_specs=[
            pl.BlockSpec((1, ts, D), lambda b, s: (b, s, 0)),
            pl.BlockSpec((1, HIST_ROWS, D), lambda b, s: (b, 0, 0)),
        ],
        out_shape=[jax.ShapeDtypeStruct((B, S, D), f32), jax.ShapeDtypeStruct((B, HIST_ROWS, D), f32)],
        scratch_shapes=[pltpu.VMEM((HIST_ROWS, D), f32)],
        compiler_params=_params(("arbitrary", "arbitrary")),
        name="pool_mix",
    )(x, hist, g.reshape(1, D), jnp.asarray(a, bf16), jnp.asarray(ah, bf16), w_pool_b, scale.reshape(1, D))


def _route_kernel(x_ref, g_ref, wr_ref, br_ref, tri_ref, h_ref, mi_ref, mf_ref, cnt_ref, run_sc, *, tm):
    i = pl.program_id(0)

    @pl.when(i == 0)
    def _():
        run_sc[...] = jnp.zeros_like(run_sc)

    h = _rms(x_ref[...], g_ref[...])
    h_ref[...] = h
    logits = jnp.dot(h, wr_ref[...], precision=lax.Precision.HIGHEST, preferred_element_type=f32) + br_ref[...]
    lane = lax.broadcasted_iota(i32, (tm, LANES), 1)
    lanef = lane.astype(f32)
    neg = jnp.float32(-jnp.inf)

    l1 = jnp.where(lane < N_GROUPS, logits, neg)
    m1 = jnp.max(l1, axis=1, keepdims=True)
    grp = jnp.min(jnp.where(l1 == m1, lanef, float(LANES)), axis=1, keepdims=True)
    p_grp = 1.0 / jnp.sum(jnp.exp(l1 - m1), axis=1, keepdims=True)

    lo = ROUTE_COL0 + EXPERTS_PER_GROUP * grp
    sel = (lanef >= lo) & (lanef < lo + EXPERTS_PER_GROUP)
    l2 = jnp.where(sel, logits, neg)
    m2 = jnp.max(l2, axis=1, keepdims=True)
    e2 = jnp.exp(l2 - m2)
    p2 = jnp.where(sel, e2 / jnp.sum(e2, axis=1, keepdims=True), -1.0)
    top0 = jnp.max(p2, axis=1, keepdims=True)
    idx0 = jnp.min(jnp.where(p2 == top0, lanef, float(LANES)), axis=1, keepdims=True)
    p2b = jnp.where(lanef == idx0, -1.0, p2)
    top1 = jnp.max(p2b, axis=1, keepdims=True)
    idx1 = jnp.min(jnp.where(p2b == top1, lanef, float(LANES)), axis=1, keepdims=True)
    den = top0 + top1
    gate0 = p_grp * top0 / den
    gate1 = p_grp * top1 / den
    ex0 = idx0 - ROUTE_COL0
    ex1 = idx1 - ROUTE_COL0

    oh0 = lanef == ex0
    oh1 = lanef == ex1
    oh = jnp.where(oh0 | oh1, 1.0, 0.0)
    before = jnp.dot(tri_ref[...], oh.astype(bf16), preferred_element_type=f32) + run_sc[...]
    rank0 = jnp.sum(jnp.where(oh0, before, 0.0), axis=1, keepdims=True)
    rank1 = jnp.sum(jnp.where(oh1, before, 0.0), axis=1, keepdims=True)
    run_sc[...] = run_sc[...] + jnp.sum(oh, axis=0, keepdims=True)
    cnt_ref[...] = run_sc[...]

    mi = jnp.where(lane == 0, ex0, jnp.where(lane == 1, ex1, jnp.where(lane == 2, rank0, jnp.where(lane == 3, rank1, 0.0))))
    mi_ref[...] = mi.astype(i32)
    mf_ref[...] = jnp.where(lane == 0, gate0, jnp.where(lane == 1, gate1, 0.0))


def _route(x, g, wr, br):
    T, D = x.shape
    tm = TOKEN_TILE
    tri = jnp.asarray(np.tril(np.ones((tm, tm), np.float32), -1), bf16)
    return pl.pallas_call(
        functools.partial(_route_kernel, tm=tm),
        grid=(T // tm,),
        in_specs=[
            pl.BlockSpec((tm, D), lambda i: (i, 0)),
            pl.BlockSpec((1, D), lambda i: (0, 0)),
            pl.BlockSpec((D, LANES), lambda i: (0, 0)),
            pl.BlockSpec((1, LANES), lambda i: (0, 0)),
            pl.BlockSpec((tm, tm), lambda i: (0, 0)),
        ],
        out_specs=[
            pl.BlockSpec((tm, D), lambda i: (i, 0)),
            pl.BlockSpec((tm, LANES), lambda i: (i, 0)),
            pl.BlockSpec((tm, LANES), lambda i: (i, 0)),
            pl.BlockSpec((1, LANES), lambda i: (0, 0)),
        ],
        out_shape=[
            jax.ShapeDtypeStruct((T, D), f32),
            jax.ShapeDtypeStruct((T, LANES), i32),
            jax.ShapeDtypeStruct((T, LANES), f32),
            jax.ShapeDtypeStruct((1, LANES), f32),
        ],
        scratch_shapes=[pltpu.VMEM((1, LANES), f32)],
        compiler_params=_params(("arbitrary",)),
        name="moe_route",
    )(x, g.reshape(1, D), wr, br, tri)


def _dispatch_kernel(dest_ref, h_ref, zin_ref, xs_ref, sem, *, tm):
    del zin_ref
    base = pl.program_id(0) * tm

    def row_copy(r, d):
        return pltpu.make_async_copy(h_ref.at[pl.ds(r, 1), :], xs_ref.at[pl.ds(d, 1), :], sem)

    def issue(r, c):
        for k in range(TOP_K):
            row_copy(r, dest_ref[TOP_K * (base + r) + k]).start()
        return c

    def drain(r, c):
        for k in range(TOP_K):
            row_copy(r, dest_ref[TOP_K * (base + r) + k]).wait()
        return c

    lax.fori_loop(0, tm, issue, 0)
    lax.fori_loop(0, tm, drain, 0)


def _dispatch(h, dest, n_rows):
    T, D = h.shape
    tm = TOKEN_TILE
    zeros = jnp.zeros((n_rows, D), f32)
    return pl.pallas_call(
        functools.partial(_dispatch_kernel, tm=tm),
        grid_spec=pltpu.PrefetchScalarGridSpec(
            num_scalar_prefetch=1,
            grid=(T // tm,),
            in_specs=[
                pl.BlockSpec((tm, D), lambda i, d: (i, 0)),
                pl.BlockSpec(memory_space=pl.ANY),
            ],
            out_specs=pl.BlockSpec(memory_space=pl.ANY),
            scratch_shapes=[pltpu.SemaphoreType.DMA(())],
        ),
        out_shape=jax.ShapeDtypeStruct((n_rows, D), f32),
        input_output_aliases={2: 0},
        compiler_params=_params(("arbitrary",)),
        name="moe_dispatch",
    )(dest, h, zeros)


def _ffn_kernel(be_ref, nu_ref, xs_ref, wg_ref, wu_ref, wd_ref, o_ref, wg_b, wu_b, wd_b):
    i = pl.program_id(0)
    prev = be_ref[jnp.maximum(i - 1, 0)]

    @pl.when((i == 0) | (be_ref[i] != prev))
    def _():
        wg_b[...] = wg_ref[0].astype(bf16)
        wu_b[...] = wu_ref[0].astype(bf16)
        wd_b[...] = wd_ref[0].astype(bf16)

    @pl.when(i < nu_ref[0])
    def _():
        xb = xs_ref[...].astype(bf16)
        gate = jnp.dot(xb, wg_b[...], preferred_element_type=f32)
        up = jnp.dot(xb, wu_b[...], preferred_element_type=f32)
        hidden = (gate * jax.nn.sigmoid(gate)) * up
        o_ref[...] = jnp.dot(hidden.astype(bf16), wd_b[...], preferred_element_type=f32)

    @pl.when(i >= nu_ref[0])
    def _():
        o_ref[...] = jnp.zeros_like(o_ref)


def _ffn(xs, blk_expert, n_used, w_gate, w_up, w_down):
    P, D = xs.shape
    DE = w_gate.shape[2]
    bm = MOE_ROWS

    def row_map(i, be, nu):
        return (jnp.minimum(i, nu[0] - 1), 0)

    return pl.pallas_call(
        _ffn_kernel,
        grid_spec=pltpu.PrefetchScalarGridSpec(
            num_scalar_prefetch=2,
            grid=(P // bm,),
            in_specs=[
                pl.BlockSpec((bm, D), row_map),
                pl.BlockSpec((1, D, DE), lambda i, be, nu: (be[i], 0, 0)),
                pl.BlockSpec((1, D, DE), lambda i, be, nu: (be[i], 0, 0)),
                pl.BlockSpec((1, DE, D), lambda i, be, nu: (be[i], 0, 0)),
            ],
            out_specs=pl.BlockSpec((bm, D), lambda i, be, nu: (i, 0)),
            scratch_shapes=[pltpu.VMEM((D, DE), bf16), pltpu.VMEM((D, DE), bf16), pltpu.VMEM((DE, D), bf16)],
        ),
        out_shape=jax.ShapeDtypeStruct((P, D), f32),
        compiler_params=_params(("arbitrary",)),
        name="moe_ffn",
    )(blk_expert, n_used, xs, w_gate, w_up, w_down)


def _combine_kernel(dest_ref, x_ref, mf_ref, ys_ref, o_ref, rows, sem, *, tm):
    base = pl.program_id(0) * tm

    def row_copy(r, k):
        d = dest_ref[TOP_K * (base + r) + k]
        return pltpu.make_async_copy(ys_ref.at[pl.ds(d, 1), :], rows.at[k, pl.ds(r, 1), :], sem)

    def issue(r, c):
        for k in range(TOP_K):
            row_copy(r, k).start()
        return c

    def drain(r, c):
        for k in range(TOP_K):
            row_copy(r, k).wait()
        return c

    lax.fori_loop(0, tm, issue, 0)
    lax.fori_loop(0, tm, drain, 0)
    mf = mf_ref[...]
    y = mf[:, 0:1] * rows[0] + mf[:, 1:2] * rows[1]
    o_ref[...] = x_ref[...] + y


def _combine(x, mf, ys, dest):
    T, D = x.shape
    tm = TOKEN_TILE
    return pl.pallas_call(
        functools.partial(_combine_kernel, tm=tm),
        grid_spec=pltpu.PrefetchScalarGridSpec(
            num_scalar_prefetch=1,
            grid=(T // tm,),
            in_specs=[
                pl.BlockSpec((tm, D), lambda i, d: (i, 0)),
                pl.BlockSpec((tm, LANES), lambda i, d: (i, 0)),
                pl.BlockSpec(memory_space=pl.ANY),
            ],
            out_specs=pl.BlockSpec((tm, D), lambda i, d: (i, 0)),
            scratch_shapes=[pltpu.VMEM((TOP_K, tm, D), f32), pltpu.SemaphoreType.DMA(())],
        ),
        out_shape=jax.ShapeDtypeStruct((T, D), f32),
        compiler_params=_params(("arbitrary",)),
        name="moe_combine",
    )(dest, x, mf, ys)


def _hier_moe(x, g, w_r1, b_r1, w_r2, b_r2, w_gate, w_up, w_down):
    T, D = x.shape
    ncol = N_GROUPS + N_EXPERTS
    wr = jnp.concatenate([w_r1, jnp.transpose(w_r2, (1, 0, 2)).reshape(D, N_EXPERTS)], axis=1)
    wr = jnp.pad(wr, ((0, 0), (0, LANES - ncol)))
    br = jnp.pad(jnp.concatenate([b_r1, b_r2.reshape(-1)]), (0, LANES - ncol)).reshape(1, LANES)
    h, mi, mf, cnt = _route(x, g, wr, br)

    bm = MOE_ROWS
    n_blocks = -(-(T * TOP_K) // bm) + N_EXPERTS
    counts = cnt[0, :N_EXPERTS].astype(i32)
    pcounts = (counts + bm - 1) // bm * bm
    pend = jnp.cumsum(pcounts)
    pstart = pend - pcounts
    dest = (pstart[mi[:, 0:TOP_K]] + mi[:, TOP_K:2 * TOP_K]).reshape(-1)
    n_used = (pend[-1] // bm).astype(i32)
    blk = jnp.arange(n_blocks, dtype=i32)
    blk_expert = jnp.searchsorted(pend, jnp.minimum(blk, n_used - 1) * bm, side='right').astype(i32)
    blk_expert = jnp.minimum(blk_expert, N_EXPERTS - 1)

    xs = _dispatch(h, dest, n_blocks * bm)
    ys = _ffn(xs, blk_expert, n_used.reshape(1), w_gate, w_up, w_down)
    return _combine(x, mf, ys, dest)


def _qkv_kernel(x_ref, g_ref, w_ref, e_ref, et_ref, qg_ref, kg_ref, q_ref, k_ref, v_ref, kb_ref, vb_ref, *, d, scale):
    hb = _rms(x_ref[...], g_ref[...]).astype(bf16)
    qkv = jnp.dot(hb, w_ref[...], preferred_element_type=f32)

    def head_norm(t, gain):
        ms = jnp.dot((t * t).astype(bf16), e_ref[...], preferred_element_type=f32)
        r = lax.rsqrt(ms + EPS)
        r_hi = r.astype(bf16)
        r_lo = (r - r_hi.astype(f32)).astype(bf16)
        rb = jnp.dot(jnp.concatenate([r_hi, r_lo], axis=1), et_ref[...], preferred_element_type=f32)
        return t * rb * gain

    q = head_norm(qkv[:, 0:d], qg_ref[...])
    k = head_norm(qkv[:, d:2 * d], kg_ref[...])
    v = qkv[:, 2 * d:3 * d]
    q_ref[...] = (q * scale).astype(bf16)
    k_ref[...] = k
    v_ref[...] = v
    kb_ref[...] = k.astype(bf16)
    vb_ref[...] = v.astype(bf16)


def _qkv(x, g, w_qkv_b, q_gain, k_gain):
    T, D = x.shape
    hd = D // N_HEADS
    tm = TOKEN_TILE
    head_of = np.arange(D) // hd
    e = (head_of[:, None] == np.arange(LANES)[None, :]).astype(np.float32) / hd
    et = (np.arange(LANES)[:, None] == head_of[None, :]).astype(np.float32)
    et2 = np.concatenate([et, et], axis=0)
    row = pl.BlockSpec((tm, D), lambda i: (i, 0))
    const = lambda shape: pl.BlockSpec(shape, lambda i: (0, 0))
    return pl.pallas_call(
        functools.partial(_qkv_kernel, d=D, scale=hd ** -0.5),
        grid=(T // tm,),
        in_specs=[row, const((1, D)), const((D, 3 * D)), const((D, LANES)), const((2 * LANES, D)), const((1, D)), const((1, D))],
        out_specs=[row, row, row, row, row],
        out_shape=[
            jax.ShapeDtypeStruct((T, D), bf16),
            jax.ShapeDtypeStruct((T, D), f32),
            jax.ShapeDtypeStruct((T, D), f32),
            jax.ShapeDtypeStruct((T, D), bf16),
            jax.ShapeDtypeStruct((T, D), bf16),
        ],
        compiler_params=_params(("arbitrary",)),
        name="attn_qkv",
    )(x, g.reshape(1, D), w_qkv_b, jnp.asarray(e, bf16), jnp.asarray(et2, bf16),
      jnp.tile(q_gain, N_HEADS).reshape(1, D), jnp.tile(k_gain, N_HEADS).reshape(1, D))


def _attn_kernel(q_ref, k_ref, v_ref, b_ref, o_ref, *, chunk0, valid_len, hd):
    c = chunk0 + pl.program_id(1)
    start = pl.multiple_of(pl.program_id(1) * CHUNK, CHUNK)
    kpos = (c - LEFT_CHUNKS) * CHUNK + lax.broadcasted_iota(i32, (CHUNK, BAND), 1)
    visible = (kpos >= 0) & (kpos < valid_len)
    q = q_ref[0]
    for h in range(N_HEADS):
        sl = slice(h * hd, (h + 1) * hd)
        kh = k_ref[0, pl.ds(start, BAND), sl]
        vh = v_ref[0, pl.ds(start, BAND), sl]
        s = lax.dot_general(q[:, sl], kh, (((1,), (1,)), ((), ())), preferred_element_type=f32)
        s = jnp.where(visible, s + b_ref[h], -1e30)
        m = jnp.max(s, axis=1, keepdims=True)
        p = jnp.exp(s - m)
        l = jnp.sum(p, axis=1, keepdims=True)
        o = jnp.dot(p.astype(bf16), vh, preferred_element_type=f32) / l
        o_ref[0, :, sl] = o.astype(bf16)


def _band_attention(q, kb, vb, bias, chunk0, valid_len):
    B, SQ, D = q.shape
    SK = kb.shape[1]
    nq = SQ // CHUNK
    return pl.pallas_call(
        functools.partial(_attn_kernel, chunk0=chunk0, valid_len=valid_len, hd=D // N_HEADS),
        grid=(B, nq),
        in_specs=[
            pl.BlockSpec((1, CHUNK, D), lambda b, c: (b, c, 0)),
            pl.BlockSpec((1, SK, D), lambda b, c: (b, 0, 0)),
            pl.BlockSpec((1, SK, D), lambda b, c: (b, 0, 0)),
            pl.BlockSpec((N_HEADS, CHUNK, BAND), lambda b, c: (0, 0, 0)),
        ],
        out_specs=pl.BlockSpec((1, CHUNK, D), lambda b, c: (b, c, 0)),
        out_shape=jax.ShapeDtypeStruct((B, SQ, D), bf16),
        compiler_params=_params(("arbitrary", "arbitrary")),
        name="band_attn",
    )(q, kb, vb, bias)


def _oproj_kernel(x_ref, o_ref, w_ref, y_ref):
    y_ref[...] = x_ref[...] + jnp.dot(o_ref[...], w_ref[...], preferred_element_type=f32)


def _oproj(x, o, w_o_b):
    T, D = x.shape
    tm = TOKEN_TILE
    row = pl.BlockSpec((tm, D), lambda i: (i, 0))
    return pl.pallas_call(
        _oproj_kernel,
        grid=(T // tm,),
        in_specs=[row, row, pl.BlockSpec((D, D), lambda i: (0, 0))],
        out_specs=row,
        out_shape=jax.ShapeDtypeStruct((T, D), f32),
        compiler_params=_params(("arbitrary",)),
        name="attn_oproj",
    )(x, o, w_o_b)


def kernel(x_prompt, x_sample, cache_pool, cache_k, cache_v, norm_mix, norm_ffn, pool_w, pool_scale, attn_w_qkv, attn_w_o, attn_q_gain, attn_k_gain, attn_rel_bias, moe_w_r1, moe_b_r1, moe_w_r2, moe_b_r2, moe_w_gate, moe_w_up, moe_w_down):
    B, S, D = x_prompt.shape
    BS, LS, _ = x_sample.shape
    past_len = 1024
    tp = B * S
    keep_p = min(LEFT_CHUNKS * CHUNK, S)
    keep_s = cache_k.shape[2]

    def moe(x, i):
        return _hier_moe(x, norm_ffn[i], moe_w_r1[i], moe_b_r1[i], moe_w_r2[i], moe_b_r2[i],
                         moe_w_gate[i], moe_w_up[i], moe_w_down[i])

    pw = pool_w[0].astype(bf16)
    hist_p = jnp.zeros((B, HIST_ROWS, D), f32)
    hist_s = jnp.pad(cache_pool[0], ((0, 0), (HIST_ROWS - POOL_HIST, 0), (0, 0)))
    xp, tail_p = _pool_layer(x_prompt, hist_p, norm_mix[0], pw, pool_scale[0], 0)
    xs, tail_s = _pool_layer(x_sample, hist_s, norm_mix[0], pw, pool_scale[0], past_len)
    state_pool_prompt = tail_p[None, :, HIST_ROWS - POOL_HIST:]
    state_pool_sample = tail_s[None, :, HIST_ROWS - POOL_HIST:]
    x = jnp.concatenate([xp.reshape(tp, D), xs.reshape(BS * LS, D)], axis=0)
    x = moe(x, 0)

    q, k, v, kb, vb = _qkv(x, norm_mix[1], attn_w_qkv[0].astype(bf16), attn_q_gain[0], attn_k_gain[0])
    rel = (LEFT_CHUNKS * CHUNK + np.arange(CHUNK)[:, None]) - np.arange(BAND)[None, :]
    bias = attn_rel_bias[0][:, np.clip(rel, -(CHUNK - 1), REL_MAX) + (CHUNK - 1)]

    front = ((0, 0), (LEFT_CHUNKS * CHUNK, 0), (0, 0))
    o_p = _band_attention(q[:tp].reshape(B, S, D), jnp.pad(kb[:tp].reshape(B, S, D), front),
                          jnp.pad(vb[:tp].reshape(B, S, D), front), bias, 0, S)
    tail = ((0, 0), (0, CHUNK - LS), (0, 0))
    kb_s = jnp.concatenate([cache_k[0].reshape(BS, keep_s, D).astype(bf16), jnp.pad(kb[tp:].reshape(BS, LS, D), tail)], axis=1)
    vb_s = jnp.concatenate([cache_v[0].reshape(BS, keep_s, D).astype(bf16), jnp.pad(vb[tp:].reshape(BS, LS, D), tail)], axis=1)
    o_s = _band_attention(jnp.pad(q[tp:].reshape(BS, LS, D), tail), kb_s, vb_s, bias, LEFT_CHUNKS, keep_s + LS)
    o = jnp.concatenate([o_p.reshape(tp, D), o_s[:, :LS].reshape(BS * LS, D)], axis=0)
    x = _oproj(x, o, attn_w_o[0].astype(bf16))
    x = moe(x, 1)

    hd = D // N_HEADS
    k_prompt = k[:tp].reshape(B, S, N_HEADS, hd)[None, :, S - keep_p:]
    v_prompt = v[:tp].reshape(B, S, N_HEADS, hd)[None, :, S - keep_p:]
    k_sample = k[tp:].reshape(1, BS, LS, N_HEADS, hd)
    v_sample = v[tp:].reshape(1, BS, LS, N_HEADS, hd)
    return (x[:tp].reshape(B, S, D), x[tp:].reshape(BS, LS, D), state_pool_prompt, state_pool_sample,
            k_prompt, v_prompt, k_sample, v_sample)
```
